```python
import jax
import jax.numpy as jnp
from jax import lax
import numpy as np

D_MODEL = 1024
BATCH = 4
SEQ = 8192
DEPTH = 4

CTX_LEN = 256
GRID_W = 64
HEAD_DIM = 64
NA_HEADS = 8
NA_WIDTH = NA_HEADS * HEAD_DIM
NA_KH = 8
NA_KW = 16
LRU_WIDTH = D_MODEL // 2
LRU_BLOCKS = 8
LRU_BLOCK_DIM = LRU_WIDTH // LRU_BLOCKS
LRU_C = 8.0
CONV_W = 4
SWA_Q_HEADS = 8
SWA_KV_HEADS = 2
SWA_Q_WIDTH = SWA_Q_HEADS * HEAD_DIM
SWA_KV_WIDTH = SWA_KV_HEADS * HEAD_DIM
SWA_WINDOW = 128
SWA_BLOCK = 128
ROPE_BASE = 10000.0
D_FF = 4 * D_MODEL
NORM_EPS = 1e-6
MASK_VALUE = -1e30
IN_SIZES = (NA_WIDTH, NA_WIDTH, NA_WIDTH, LRU_WIDTH, LRU_WIDTH, SWA_Q_WIDTH, SWA_KV_WIDTH, SWA_KV_WIDTH, D_MODEL, D_MODEL, D_MODEL)
IN_SPLITS = tuple(sum(IN_SIZES[:i + 1]) for i in range(len(IN_SIZES) - 1))
IN_WIDTH = sum(IN_SIZES)
BRANCH_SIZES = (NA_WIDTH, LRU_WIDTH, SWA_Q_WIDTH)
BRANCH_SPLITS = (NA_WIDTH, NA_WIDTH + LRU_WIDTH)
MIX_WIDTH = sum(BRANCH_SIZES)

kernel_name = 'hybrid_natten_rglru_swa_prefix_dit'


def rms_norm(x, g):
    xf = x.astype(jnp.float32)
    y = xf * lax.rsqrt(jnp.mean(xf * xf, axis=-1, keepdims=True) + NORM_EPS)
    return (y * g.astype(jnp.float32)).astype(x.dtype)


def modulate(h, shift, scale):
    return h * (1 + scale) + shift


def heads(t, n):
    return t.reshape(*t.shape[:-1], n, HEAD_DIM)


def axial_rope(n_tokens):
    t = jnp.arange(n_tokens, dtype=jnp.int32)
    row = (t // GRID_W).astype(jnp.float32)
    col = (t % GRID_W).astype(jnp.float32)
    n_freq = HEAD_DIM // 4
    inv_freq = ROPE_BASE ** (-jnp.arange(n_freq, dtype=jnp.float32) / n_freq)
    ang = jnp.concatenate([row[:, None] * inv_freq, col[:, None] * inv_freq], axis=-1)
    return jnp.cos(ang), jnp.sin(ang)


def apply_rope(t, cos, sin):
    tf = t.astype(jnp.float32)
    half = HEAD_DIM // 2
    t1, t2 = tf[..., :half], tf[..., half:]
    cs, sn = cos[None, :, None, :], sin[None, :, None, :]
    return jnp.concatenate([t1 * cs - t2 * sn, t2 * cs + t1 * sn], axis=-1).astype(t.dtype)


def context_attention(q, k, v, sink):
    B, T, H, d = q.shape
    G = k.shape[2]
    R = H // G
    qg = q.reshape(B, T, G, R, d)
    s = jnp.einsum('btgrd,bsgd->bgrts', qg, k, preferred_element_type=jnp.float32) * (d ** -0.5)
    if sink is not None:
        sk = jnp.broadcast_to(sink.astype(jnp.float32).reshape(1, G, R, 1, 1), (B, G, R, T, 1))
        s = jnp.concatenate([s, sk], axis=-1)
    p = jax.nn.softmax(s, axis=-1)
    if sink is not None:
        p = p[..., :-1]
    o = jnp.einsum('bgrts,bsgd->btgrd', p.astype(v.dtype), v)
    return o.reshape(B, T, H, d)


def neighbourhood_attention(q, k, v, kc, vc, rpb):
    B, S, H, d = q.shape
    rows = S // GRID_W
    kh = min(NA_KH, rows)
    kw = NA_KW
    scale = d ** -0.5
    qg = q.reshape(B, rows, GRID_W, H, d)
    kg = k.reshape(B, rows, GRID_W, H, d)
    vg = v.reshape(B, rows, GRID_W, H, d)
    col = np.arange(GRID_W)
    col_start = np.clip(col - kw // 2, 0, GRID_W - kw)
    col_idx = col_start[:, None] + np.arange(kw)[None, :]
    dc = col_idx - col[:, None] + (NA_KW - 1)
    rpb_c = rpb[:, :, dc]

    def one_row(r):
        rs = jnp.clip(r - kh // 2, 0, rows - kh)
        kn = lax.dynamic_slice_in_dim(kg, rs, kh, axis=1)[:, :, col_idx]
        vn = lax.dynamic_slice_in_dim(vg, rs, kh, axis=1)[:, :, col_idx]
        qr = lax.dynamic_index_in_dim(qg, r, axis=1, keepdims=False)
        s_loc = jnp.einsum('bwhd,biwjhd->bhwij', qr, kn, preferred_element_type=jnp.float32) * scale
        dr = rs + jnp.arange(kh) - r + (NA_KH - 1)
        bias = jnp.take(rpb_c, dr, axis=1).transpose(0, 2, 1, 3)
        s_loc = s_loc + bias.astype(jnp.float32)[None]
        s_ctx = jnp.einsum('bwhd,bchd->bhwc', qr, kc, preferred_element_type=jnp.float32) * scale
        logits = jnp.concatenate([s_loc.reshape(B, H, GRID_W, kh * kw), s_ctx], axis=-1)
        p = jax.nn.softmax(logits, axis=-1)
        p_loc = p[..., :kh * kw].reshape(B, H, GRID_W, kh, kw).astype(v.dtype)
        p_ctx = p[..., kh * kw:].astype(v.dtype)
        return (jnp.einsum('bhwij,biwjhd->bwhd', p_loc, vn)
                + jnp.einsum('bhwc,bchd->bwhd', p_ctx, vc))

    out = lax.map(one_row, jnp.arange(rows))
    return out.transpose(1, 0, 2, 3, 4).reshape(B, S, H, d)


def sliding_window_attention(q, k, v, kc, vc, sink):
    B, S, H, d = q.shape
    G = k.shape[2]
    R = H // G
    blk = SWA_BLOCK
    nb = S // blk
    qb = q.reshape(B, nb, blk, G, R, d)

    def band(t):
        tp = jnp.pad(t, ((0, 0), (blk, blk), (0, 0), (0, 0))).reshape(B, nb + 2, blk, G, d)
        return jnp.concatenate([tp[:, :-2], tp[:, 1:-1], tp[:, 2:]], axis=2)

    kb, vb = band(k), band(v)
    qpos = np.arange(blk)[:, None]
    kpos = np.arange(3 * blk)[None, :] - blk
    kabs = np.arange(nb)[:, None] * blk + kpos
    valid = ((np.abs(kpos - qpos) <= SWA_WINDOW)[None]
             & ((kabs >= 0) & (kabs < S))[:, None, :])
    scale = d ** -0.5
    s_loc = jnp.einsum('bnqgrd,bnkgd->bngrqk', qb, kb, preferred_element_type=jnp.float32) * scale
    s_loc = jnp.where(valid[None, :, None, None], s_loc, MASK_VALUE)
    s_ctx = jnp.einsum('bnqgrd,bcgd->bngrqc', qb, kc, preferred_element_type=jnp.float32) * scale
    sk = jnp.broadcast_to(sink.astype(jnp.float32).reshape(1, 1, G, R, 1, 1), (B, nb, G, R, blk, 1))
    p = jax.nn.softmax(jnp.concatenate([s_loc, s_ctx, sk], axis=-1), axis=-1)
    n_loc = 3 * blk
    p_loc = p[..., :n_loc].astype(v.dtype)
    p_ctx = p[..., n_loc:-1].astype(v.dtype)
    o = (jnp.einsum('bngrqk,bnkgd->bnqgrd', p_loc, vb)
         + jnp.einsum('bngrqc,bcgd->bnqgrd', p_ctx, vc))
    return o.reshape(B, S, H, d)


def centred_dwconv(x, w, b):
    T = x.shape[1]
    left = CONV_W // 2
    right = CONV_W - 1 - left
    xp = jnp.pad(x, ((0, 0), (left, right), (0, 0)))
    y = xp[:, 0:T] * w[0]
    for i in range(1, CONV_W):
        y = y + xp[:, i:i + T] * w[i]
    return y + b


def block_diag_linear(x, w, b):
    xb = x.reshape(*x.shape[:-1], LRU_BLOCKS, LRU_BLOCK_DIM)
    y = jnp.einsum('...nc,ncd->...nd', xb, w)
    return y.reshape(x.shape) + b


def rglru_coeffs(u, w_a, b_a, w_x, b_x, lam):
    r = jax.nn.sigmoid(block_diag_linear(u, w_a, b_a).astype(jnp.float32))
    i = jax.nn.sigmoid(block_diag_linear(u, w_x, b_x).astype(jnp.float32))
    log_a = -LRU_C * r * jax.nn.softplus(-lam.astype(jnp.float32))
    a = jnp.exp(log_a)
    mult = jnp.sqrt(-jnp.expm1(2.0 * log_a))
    return a, mult * i * u.astype(jnp.float32)


def linear_scan(a, b, h0, reverse):
    if reverse:
        a, b = jnp.flip(a, axis=1), jnp.flip(b, axis=1)
    b = b.at[:, 0].add(a[:, 0] * h0)

    def combine(l, r):
        return l[0] * r[0], r[0] * l[1] + r[1]

    _, h = lax.associative_scan(combine, (a, b), axis=1)
    return jnp.flip(h, axis=1) if reverse else h


def rg_lru_bidirectional(u_lat, u_ctx, w_a, b_a, w_x, b_x, lam):
    B, _, C = u_lat.shape
    y_lat = jnp.zeros(u_lat.shape, jnp.float32)
    y_ctx = jnp.zeros(u_ctx.shape, jnp.float32)
    for d in range(2):
        rev = d == 1
        a_c, b_c = rglru_coeffs(u_ctx, w_a[d], b_a[d], w_x[d], b_x[d], lam[d])
        h_c = linear_scan(a_c, b_c, jnp.zeros((B, C), jnp.float32), rev)
        h_c_final = h_c[:, 0] if rev else h_c[:, -1]
        a_l, b_l = rglru_coeffs(u_lat, w_a[d], b_a[d], w_x[d], b_x[d], lam[d])
        h_l = linear_scan(a_l, b_l, h_c_final, rev)
        y_lat = y_lat + h_l
        y_ctx = y_ctx + h_c
    return y_lat, y_ctx


def merge_branches(ys, gates, w_branch, w_out):
    w_parts = jnp.split(w_branch, BRANCH_SPLITS, axis=0)
    m = jax.nn.sigmoid(gates[0]) * (ys[0] @ w_parts[0])
    for y, g, w in zip(ys[1:], gates[1:], w_parts[1:]):
        m = m + jax.nn.sigmoid(g) * (y @ w)
    return m @ w_out


def sq_relu_mlp(h, w1, w2):
    return jnp.square(jax.nn.relu(h @ w1)) @ w2


def setup_inputs(seed: int = 0) -> dict:
    key = jax.random.key(seed)
    ks = jax.random.split(key, 24)
    f32 = jnp.float32
    L = DEPTH

    def nrm(k, shape, s):
        return s * jax.random.normal(k, shape, f32)

    u = jax.random.uniform(ks[16], (L, 2, LRU_WIDTH), f32, 0.9, 0.999)
    a0 = u ** (1.0 / LRU_C)
    lru_lambda = jnp.log(a0) - jnp.log1p(-a0)
    return {
        'x': nrm(ks[0], (BATCH, SEQ, D_MODEL), 1.0),
        'c': nrm(ks[1], (BATCH, D_MODEL), 1.0),
        'ctx': nrm(ks[2], (BATCH, CTX_LEN, D_MODEL), 1.0),
        'c_ctx': nrm(ks[3], (D_MODEL,), 1.0),
        'w_mod': nrm(ks[4], (L, D_MODEL, 6 * D_MODEL), 0.5 * D_MODEL ** -0.5),
        'b_mod': nrm(ks[5], (L, 6 * D_MODEL), 0.02),
        'norm1_g': 1.0 + nrm(ks[6], (L, D_MODEL), 0.05),
        'norm2_g': 1.0 + nrm(ks[7], (L, D_MODEL), 0.05),
        'w_in': nrm(ks[8], (L, D_MODEL, IN_WIDTH), D_MODEL ** -0.5),
        'na_rpb': nrm(ks[9], (L, NA_HEADS, 2 * NA_KH - 1, 2 * NA_KW - 1), 0.1),
        'conv_w': nrm(ks[10], (L, CONV_W, LRU_WIDTH), CONV_W ** -0.5),
        'conv_b': nrm(ks[11], (L, LRU_WIDTH), 0.01),
        'lru_wa': nrm(ks[12], (L, 2, LRU_BLOCKS, LRU_BLOCK_DIM, LRU_BLOCK_DIM), LRU_BLOCK_DIM ** -0.5),
        'lru_ba': nrm(ks[13], (L, 2, LRU_WIDTH), 0.01),
        'lru_wx': nrm(ks[14], (L, 2, LRU_BLOCKS, LRU_BLOCK_DIM, LRU_BLOCK_DIM), LRU_BLOCK_DIM ** -0.5),
        'lru_bx': nrm(ks[15], (L, 2, LRU_WIDTH), 0.01),
        'lru_lambda': lru_lambda,
        'swa_sink': nrm(ks[17], (L, SWA_Q_HEADS), 0.5),
        'w_branch': nrm(ks[18], (L, MIX_WIDTH, D_MODEL), NA_WIDTH ** -0.5),
        'w_out': nrm(ks[19], (L, D_MODEL, D_MODEL), D_MODEL ** -0.5),
        'w_ff1': nrm(ks[20], (L, D_MODEL, D_FF), D_MODEL ** -0.5),
        'w_ff2': nrm(ks[21], (L, D_FF, D_MODEL), D_FF ** -0.5),
        'final_g': 1.0 + nrm(ks[22], (D_MODEL,), 0.05),
    }


def reference(x, c, ctx, c_ctx, w_mod, b_mod, norm1_g, norm2_g, w_in, na_rpb, conv_w, conv_b,
              lru_wa, lru_ba, lru_wx, lru_bx, lru_lambda, swa_sink, w_branch, w_out, w_ff1, w_ff2, final_g):
    dt = x.dtype
    B, S, _ = x.shape
    rope_cos, rope_sin = axial_rope(S)
    cond_lat = jax.nn.silu(c)[:, None, :]
    cond_ctx = jax.nn.silu(c_ctx)[None, None, :]
    xc = ctx
    for l in range(DEPTH):
        update_ctx = l < DEPTH - 1
        mod_l = jnp.split(cond_lat @ w_mod[l] + b_mod[l], 6, axis=-1)
        mod_c = jnp.split(cond_ctx @ w_mod[l] + b_mod[l], 6, axis=-1)

        h = modulate(rms_norm(x, norm1_g[l]), mod_l[0], mod_l[1])
        hc = modulate(rms_norm(xc, norm1_g[l]), mod_c[0], mod_c[1])
        qa, ka, va, xb, gb, qs, ks, vs, ga, gr, gs = jnp.split(h @ w_in[l], IN_SPLITS, axis=-1)
        qac, kac, vac, xbc, gbc, qsc, ksc, vsc, gac, grc, gsc = jnp.split(hc @ w_in[l], IN_SPLITS, axis=-1)

        kac_h, vac_h = heads(kac, NA_HEADS), heads(vac, NA_HEADS)
        y_a = neighbourhood_attention(heads(qa, NA_HEADS), heads(ka, NA_HEADS), heads(va, NA_HEADS),
                                      kac_h, vac_h, na_rpb[l]).reshape(B, S, NA_WIDTH)

        u_lat = centred_dwconv(xb, conv_w[l], conv_b[l])
        u_ctx = centred_dwconv(xbc, conv_w[l], conv_b[l])
        r_lat, r_ctx = rg_lru_bidirectional(u_lat, u_ctx, lru_wa[l], lru_ba[l], lru_wx[l], lru_bx[l], lru_lambda[l])
        y_b = r_lat.astype(dt) * jax.nn.gelu(gb)

        ksc_h, vsc_h = heads(ksc, SWA_KV_HEADS), heads(vsc, SWA_KV_HEADS)
        qs_r = apply_rope(heads(qs, SWA_Q_HEADS), rope_cos, rope_sin)
        ks_r = apply_rope(heads(ks, SWA_KV_HEADS), rope_cos, rope_sin)
        y_c = sliding_window_attention(qs_r, ks_r, heads(vs, SWA_KV_HEADS), ksc_h, vsc_h,
                                       swa_sink[l]).reshape(B, S, SWA_Q_WIDTH)

        x = x + mod_l[2] * merge_branches((y_a, y_b, y_c), (ga, gr, gs), w_branch[l], w_out[l])
        if update_ctx:
            y_ac = context_attention(heads(qac, NA_HEADS), kac_h, vac_h, None).reshape(B, -1, NA_WIDTH)
            y_bc = r_ctx.astype(dt) * jax.nn.gelu(gbc)
            y_cc = context_attention(heads(qsc, SWA_Q_HEADS), ksc_h, vsc_h, swa_sink[l]).reshape(B, -1, SWA_Q_WIDTH)
            xc = xc + mod_c[2] * merge_branches((y_ac, y_bc, y_cc), (gac, grc, gsc), w_branch[l], w_out[l])

        h2 = modulate(rms_norm(x, norm2_g[l]), mod_l[3], mod_l[4])
        x = x + mod_l[5] * sq_relu_mlp(h2, w_ff1[l], w_ff2[l])
        if update_ctx:
            h2c = modulate(rms_norm(xc, norm2_g[l]), mod_c[3], mod_c[4])
            xc = xc + mod_c[5] * sq_relu_mlp(h2c, w_ff1[l], w_ff2[l])
    return rms_norm(x, final_g)
```

```python
import functools

import numpy as np
import jax
import jax.numpy as jnp
from jax import lax
from jax.experimental import pallas as pl
from jax.experimental.pallas import tpu as pltpu

F32 = jnp.float32
BF16 = jnp.bfloat16

HEAD_DIM = 64
GRID_W = 64
NA_HEADS = 8
NA_KH = 8
NA_KW = 16
LRU_BLOCKS = 8
LRU_C = 8.0
CONV_W = 4
SWA_Q_HEADS = 8
SWA_KV_HEADS = 2
SWA_WINDOW = 128
ROPE_BASE = 10000.0
NORM_EPS = 1e-6
MASK_VALUE = -1e30

LANES = 128
SUBLANES = 8
V7X_VMEM_BYTES = 64 * 1024 * 1024
VMEM_LIMIT = V7X_VMEM_BYTES - 8 * 1024 * 1024

TM_PROJ = 512
NA_QROWS = 8
NA_KROWS = 16
NA_KBLK = 4
SWA_BLK = 128
LRU_TB = 256
COL_CHUNK = 512


def _cparams(n_axes):
    return pltpu.CompilerParams(dimension_semantics=("arbitrary",) * n_axes,
                                vmem_limit_bytes=VMEM_LIMIT)


def _resident(shape):
    return pl.BlockSpec(shape, lambda *_: (0,) * len(shape), pipeline_mode=pl.Buffered(1))


def _rms(x):
    return x * lax.rsqrt(jnp.mean(x * x, axis=-1, keepdims=True) + NORM_EPS)


def _mod_kernel(c_ref, w_ref, b_ref, o_ref):
    c = c_ref[...]
    s = c * jax.nn.sigmoid(c)
    o_ref[...] = jnp.dot(s, w_ref[...], preferred_element_type=F32) + b_ref[...]


def _mod_call(cond, w_mod, b_mod):
    L, D, D6 = w_mod.shape
    n = D6 // D
    return pl.pallas_call(
        _mod_kernel,
        grid=(L, n),
        in_specs=[
            pl.BlockSpec((SUBLANES, D), lambda l, j: (0, 0)),
            pl.BlockSpec((None, D, D), lambda l, j: (l, 0, j)),
            pl.BlockSpec((None, 1, D), lambda l, j: (l, 0, j)),
        ],
        out_specs=pl.BlockSpec((None, SUBLANES, D), lambda l, j: (l, 0, j)),
        out_shape=jax.ShapeDtypeStruct((L, SUBLANES, D6), F32),
        compiler_params=_cparams(2),
        name="mod",
    )(cond, w_mod, b_mod.reshape(L, 1, D6))


_PROJ_SEGS = (
    ("qa", 512, BF16, False), ("ka", 512, BF16, False), ("va", 512, BF16, False),
    ("xb", 512, F32, False), ("gb", 512, F32, False),
    ("qs", 512, BF16, True), ("ks", 256, BF16, True), ("vs", 256, BF16, False),
    ("gates", 3072, F32, False),
)
_PROJ_WIDTH = sum(s[1] for s in _PROJ_SEGS)


def _rope(z, cos, sin):
    w = z.shape[1]
    lane = lax.broadcasted_iota(jnp.int32, (1, w), 1)
    first_half = (lane & (HEAD_DIM - 1)) < HEAD_DIM // 2
    swapped = jnp.where(first_half, pltpu.roll(z, w - HEAD_DIM // 2, 1), pltpu.roll(z, HEAD_DIM // 2, 1))
    reps = w // LANES
    return z * jnp.tile(cos, (1, reps)) + swapped * jnp.tile(sin, (1, reps))


def _inproj_kernel(*refs, rope):
    if rope:
        x_ref, shift_ref, scale_ref, g_ref, cos_ref, sin_ref, w_ref = refs[:7]
        outs = refs[7:]
    else:
        x_ref, shift_ref, scale_ref, g_ref, w_ref = refs[:5]
        outs = refs[5:]
    h = (_rms(x_ref[...]) * g_ref[...]) * (1.0 + scale_ref[...]) + shift_ref[...]
    hb = h.astype(BF16)
    col = 0
    for (_, width, _, rotary), o_ref in zip(_PROJ_SEGS, outs):
        for c0 in range(0, width, COL_CHUNK):
            cw = min(COL_CHUNK, width - c0)
            z = jnp.dot(hb, w_ref[:, col + c0:col + c0 + cw], preferred_element_type=F32)
            if rotary and rope:
                z = _rope(z, cos_ref[...], sin_ref[...])
            o_ref[:, c0:c0 + cw] = z.astype(o_ref.dtype)
        col += width


def _inproj_call(x, shift, scale, g, w, rope_tabs, tokens_per_seq, cond_row):
    N, D = x.shape
    tm = min(TM_PROJ, tokens_per_seq)
    blocks_per_seq = tokens_per_seq // tm
    rope = rope_tabs is not None
    in_specs = [
        pl.BlockSpec((tm, D), lambda i: (i, 0)),
        pl.BlockSpec((None, 1, D), lambda i: (cond_row(i, blocks_per_seq), 0, 0)),
        pl.BlockSpec((None, 1, D), lambda i: (cond_row(i, blocks_per_seq), 0, 0)),
        _resident((1, D)),
    ]
    args = [x, shift, scale, g]
    if rope:
        in_specs += [pl.BlockSpec((tm, LANES), lambda i: (i % blocks_per_seq, 0))] * 2
        args += list(rope_tabs)
    in_specs.append(_resident((D, _PROJ_WIDTH)))
    args.append(w)
    out_specs = [pl.BlockSpec((tm, width), lambda i: (i, 0)) for _, width, _, _ in _PROJ_SEGS]
    out_shape = [jax.ShapeDtypeStruct((N, width), dt) for _, width, dt, _ in _PROJ_SEGS]
    return pl.pallas_call(
        functools.partial(_inproj_kernel, rope=rope),
        grid=(N // tm,),
        in_specs=in_specs,
        out_specs=out_specs,
        out_shape=out_shape,
        compiler_params=_cparams(1),
        name="inproj_rope" if rope else "inproj",
    )(*args)


def _qk(q, k):
    return lax.dot_general(q, k, (((1,), (1,)), ((), ())), preferred_element_type=F32)


def _softmax_pv(logits, vals, sink=None):
    mx = None
    for s in logits:
        r = jnp.max(s, axis=-1, keepdims=True)
        mx = r if mx is None else jnp.maximum(mx, r)
    if sink is not None:
        mx = jnp.maximum(mx, sink)
    denom = None
    acc = None
    for s, v in zip(logits, vals):
        p = jnp.exp(s - mx)
        ps = jnp.sum(p, axis=-1, keepdims=True)
        pv = jnp.dot(p.astype(BF16), v, preferred_element_type=F32)
        denom = ps if denom is None else denom + ps
        acc = pv if acc is None else acc + pv
    if sink is not None:
        denom = denom + jnp.exp(sink - mx)
    return acc * (1.0 / denom)


def _low_half():
    return lax.broadcasted_iota(jnp.int32, (1, LANES), 1) < HEAD_DIM


def _na_bias_plan(tile_type):
    plan = {}
    for a in range(NA_QROWS):
        for b in range(NA_KROWS):
            if tile_type == 0:
                first, dr = max(a - NA_KH // 2, 0), b - a + NA_KH - 1
            elif tile_type == 1:
                first, dr = a, b - a + NA_KH // 2 - 1
            else:
                first, dr = NA_QROWS + min(a - NA_KH // 2, 0), b - a - 1
            plan[(a, b)] = dr if first <= b < first + NA_KH else None
    return plan


def _na_assemble_bias(tile_type, tz_ref, bias_scr, low):
    plan = _na_bias_plan(tile_type)
    masked = jnp.full((GRID_W, LANES), MASK_VALUE, F32)
    for e in range(2):
        for a in range(NA_QROWS):
            for bb in range(NA_KROWS // 2):
                d0, d1 = plan[(a, 2 * bb)], plan[(a, 2 * bb + 1)]
                if d0 is None and d1 is None:
                    val = masked
                else:
                    t0 = masked if d0 is None else tz_ref[e, d0]
                    t1 = masked if d1 is None else tz_ref[e, d1]
                    val = jnp.where(low, t0, t1)
                bias_scr[e, a * GRID_W:(a + 1) * GRID_W, bb * LANES:(bb + 1) * LANES] = val


def _na_kernel(*refs, n_tiles, local):
    if local:
        q_ref = refs[0]
        k_refs, v_refs = refs[1:5], refs[5:9]
        kc_ref, vc_ref, tz_ref, o_ref, bias_scr = refs[9:]
    else:
        q_ref, kc_ref, vc_ref, o_ref = refs
    low = _low_half()
    if local:
        m = pl.program_id(2)

        @pl.when(m == 0)
        def _():
            _na_assemble_bias(0, tz_ref, bias_scr, low)

        if n_tiles > 2:
            @pl.when(m == 1)
            def _():
                _na_assemble_bias(1, tz_ref, bias_scr, low)

        @pl.when(m == n_tiles - 1)
        def _():
            _na_assemble_bias(2, tz_ref, bias_scr, low)

    q = q_ref[...]
    kc = kc_ref[...]
    vc = vc_ref[...]
    kblk = NA_KBLK * GRID_W
    outs = []
    for e in range(2):
        sel = low if e == 0 else jnp.logical_not(low)
        qm = jnp.where(sel, q, jnp.zeros_like(q))
        logits, vals = [], []
        if local:
            for j in range(len(k_refs)):
                logits.append(_qk(qm, k_refs[j][...]) + bias_scr[e, :, j * kblk:(j + 1) * kblk])
                vals.append(v_refs[j][...])
        logits.append(_qk(qm, kc))
        vals.append(vc)
        outs.append(_softmax_pv(logits, vals))
    o_ref[...] = jnp.where(low, outs[0], outs[1]).astype(o_ref.dtype)


def _na_call(q, k, v, kc, vc, tz, B, S, C):
    rows = S // GRID_W
    n_tiles = rows // NA_QROWS
    tq = NA_QROWS * GRID_W
    kblk = NA_KBLK * GRID_W
    nkb = NA_KROWS // NA_KBLK
    n_kblocks = S // kblk
    pairs = NA_HEADS // 2

    def kv_spec(j):
        def idx(b, p, m):
            first = jnp.clip(2 * m - 1, 0, n_kblocks - nkb)
            return (b * n_kblocks + first + j, p)
        return pl.BlockSpec((kblk, LANES), idx)

    q_spec = pl.BlockSpec((tq, LANES), lambda b, p, m: (b * n_tiles + m, p))
    c_spec = pl.BlockSpec((C, LANES), lambda b, p, m: (b, p))
    in_specs = ([q_spec] + [kv_spec(j) for j in range(nkb)] * 1 + [kv_spec(j) for j in range(nkb)]
                + [c_spec, c_spec,
                   pl.BlockSpec((None, 2, 2 * NA_KH - 1, GRID_W, LANES), lambda b, p, m: (p, 0, 0, 0, 0))])
    return pl.pallas_call(
        functools.partial(_na_kernel, n_tiles=n_tiles, local=True),
        grid=(B, pairs, n_tiles),
        in_specs=in_specs,
        out_specs=q_spec,
        out_shape=jax.ShapeDtypeStruct((B * S, NA_HEADS * HEAD_DIM), BF16),
        scratch_shapes=[pltpu.VMEM((2, tq, NA_KROWS * GRID_W), F32)],
        compiler_params=_cparams(3),
        name="na_attn",
    )(q, *([k] * nkb), *([v] * nkb), kc, vc, tz)


def _na_ctx_call(qc, kc, vc, B, C):
    pairs = NA_HEADS // 2
    spec = pl.BlockSpec((C, LANES), lambda b, p: (b, p))
    return pl.pallas_call(
        functools.partial(_na_kernel, n_tiles=1, local=False),
        grid=(B, pairs),
        in_specs=[spec, spec, spec],
        out_specs=spec,
        out_shape=jax.ShapeDtypeStruct((B * C, NA_HEADS * HEAD_DIM), BF16),
        compiler_params=_cparams(2),
        name="na_ctx_attn",
    )(qc, kc, vc)


def _swa_kernel(*refs, n_blocks, local):
    if local:
        q_ref = refs[0]
        k_refs, v_refs = refs[1:4], refs[4:7]
        kc_ref, vc_ref, band_ref, sink_ref, o_ref = refs[7:]
    else:
        q_ref, kc_ref, vc_ref, sink_ref, o_ref = refs
    low = _low_half()
    q = q_ref[...]
    tq = q.shape[0]
    group = SWA_Q_HEADS // SWA_KV_HEADS
    parts = []
    for r in range(group):
        qp = q[:, (r // 2) * LANES:(r // 2 + 1) * LANES]
        sel = low if r % 2 == 0 else jnp.logical_not(low)
        parts.append(jnp.where(sel, qp, jnp.zeros_like(qp)))
    qst = jnp.concatenate(parts, axis=0)
    logits, vals = [], []
    if local:
        n = pl.program_id(2)
        band = band_ref[...]
        for j in range(3):
            s = _qk(qst, k_refs[j][...]) + jnp.tile(band[:, j * SWA_BLK:(j + 1) * SWA_BLK], (group, 1))
            if j == 0:
                s = s + jnp.where(n == 0, MASK_VALUE, 0.0)
            if j == 2:
                s = s + jnp.where(n == n_blocks - 1, MASK_VALUE, 0.0)
            logits.append(s)
            vals.append(v_refs[j][...])
    logits.append(_qk(qst, kc_ref[...]))
    vals.append(vc_ref[...])
    o = _softmax_pv(logits, vals, sink=sink_ref[...][:, :1])
    for jp in range(group // 2):
        o0 = o[(2 * jp) * tq:(2 * jp + 1) * tq]
        o1 = o[(2 * jp + 1) * tq:(2 * jp + 2) * tq]
        o_ref[:, jp * LANES:(jp + 1) * LANES] = jnp.where(low, o0, o1).astype(o_ref.dtype)


def _swa_call(q, k, v, kc, vc, band, sink, B, S, C):
    nb = S // SWA_BLK
    gw = (SWA_Q_HEADS // SWA_KV_HEADS) * HEAD_DIM

    def kv_spec(j):
        return pl.BlockSpec((SWA_BLK, LANES), lambda b, g, n: (b * nb + jnp.clip(n + j - 1, 0, nb - 1), g))

    q_spec = pl.BlockSpec((SWA_BLK, gw), lambda b, g, n: (b * nb + n, g))
    c_spec = pl.BlockSpec((C, LANES), lambda b, g, n: (b, g))
    in_specs = ([q_spec] + [kv_spec(j) for j in range(3)] + [kv_spec(j) for j in range(3)]
                + [c_spec, c_spec,
                   pl.BlockSpec((SWA_BLK, 3 * SWA_BLK), lambda b, g, n: (0, 0)),
                   pl.BlockSpec((None, sink.shape[1], LANES), lambda b, g, n: (g, 0, 0))])
    return pl.pallas_call(
        functools.partial(_swa_kernel, n_blocks=nb, local=True),
        grid=(B, SWA_KV_HEADS, nb),
        in_specs=in_specs,
        out_specs=q_spec,
        out_shape=jax.ShapeDtypeStruct((B * S, SWA_Q_HEADS * HEAD_DIM), BF16),
        compiler_params=_cparams(3),
        name="swa_attn",
    )(q, k, k, k, v, v, v, kc, vc, band, sink)


def _swa_ctx_call(qc, kc, vc, sink, B, C):
    gw = (SWA_Q_HEADS // SWA_KV_HEADS) * HEAD_DIM
    q_spec = pl.BlockSpec((C, gw), lambda b, g: (b, g))
    c_spec = pl.BlockSpec((C, LANES), lambda b, g: (b, g))
    return pl.pallas_call(
        functools.partial(_swa_kernel, n_blocks=1, local=False),
        grid=(B, SWA_KV_HEADS),
        in_specs=[q_spec, c_spec, c_spec,
                  pl.BlockSpec((None, sink.shape[1], LANES), lambda b, g: (g, 0, 0))],
        out_specs=q_spec,
        out_shape=jax.ShapeDtypeStruct((B * C, SWA_Q_HEADS * HEAD_DIM), BF16),
        compiler_params=_cparams(2),
        name="swa_ctx_attn",
    )(qc, kc, vc, sink)


def _scan_block(a, b, carry, reverse):
    tb = a.shape[0]
    row = lax.broadcasted_iota(jnp.int32, (tb, 1), 0) & (SUBLANES - 1)
    A, Bv = a, b
    for s in (1, 2, 4):
        if reverse:
            inside, shift = row < SUBLANES - s, tb - s
        else:
            inside, shift = row >= s, s
        a_sh = jnp.where(inside, pltpu.roll(A, shift, 0), 1.0)
        b_sh = jnp.where(inside, pltpu.roll(Bv, shift, 0), 0.0)
        Bv = A * b_sh + Bv
        A = A * a_sh
    n_groups = tb // SUBLANES
    hs = [None] * n_groups
    for k in (reversed(range(n_groups)) if reverse else range(n_groups)):
        hk = A[k * SUBLANES:(k + 1) * SUBLANES] * carry + Bv[k * SUBLANES:(k + 1) * SUBLANES]
        hs[k] = hk
        carry = hk[0:1] if reverse else hk[SUBLANES - 1:SUBLANES]
    return jnp.concatenate(hs, axis=0), carry


def _lru_kernel(*refs, n_blocks, reverse, fuse):
    x_ref, xp_ref, xn_ref, h0_ref, cw_ref, cb_ref, w_ref, ba_ref, bx_ref, lam_ref = refs[:10]
    if fuse:
        hf_ref, gb_ref, y_ref, st_ref, carry_ref = refs[10:]
    else:
        y_ref, st_ref, carry_ref = refs[10:]
    j = pl.program_id(1)
    blk = (n_blocks - 1 - j) if reverse else j

    @pl.when(j == 0)
    def _():
        carry_ref[...] = h0_ref[...]

    x = x_ref[...]
    tb = x.shape[0]
    xp = jnp.where(blk > 0, xp_ref[...], 0.0)
    xn = jnp.where(blk < n_blocks - 1, xn_ref[...], 0.0)
    xc = jnp.concatenate([xp, x, xn], axis=0)
    cw = cw_ref[...]
    base = SUBLANES - CONV_W // 2
    u = xc[base:base + tb] * cw[0:1]
    for i in range(1, CONV_W):
        u = u + xc[base + i:base + i + tb] * cw[i:i + 1]
    u = u + cb_ref[...]
    ub = u.astype(BF16)

    z = -lam_ref[...]
    softplus = jnp.maximum(z, 0.0) + jnp.log1p(jnp.exp(-jnp.abs(z)))
    half = x.shape[1] // 2
    for g in range(2):
        cols = slice(g * half, (g + 1) * half)
        zz = jnp.dot(ub[:, cols], w_ref[g], preferred_element_type=F32)
        r = jax.nn.sigmoid(zz[:, :half] + ba_ref[:, cols])
        gate_i = jax.nn.sigmoid(zz[:, half:] + bx_ref[:, cols])
        log_a = (-LRU_C * r) * softplus[:, cols]
        a = jnp.exp(log_a)
        bcoef = jnp.sqrt(1.0 - a * a) * gate_i * u[:, cols]
        h, carry = _scan_block(a, bcoef, carry_ref[:, cols], reverse)
        carry_ref[:, cols] = carry
        st_ref[:, cols] = carry
        if fuse:
            y_ref[:, cols] = ((hf_ref[:, cols] + h) * jax.nn.gelu(gb_ref[:, cols])).astype(y_ref.dtype)
        else:
            y_ref[:, cols] = h


def _lru_call(xb, h0, cw, cb, w, ba, bx, lam, reverse, fuse_with=None):
    B, T, W = xb.shape
    tb = min(LRU_TB, T)
    nb = T // tb
    halo_per_blk = tb // SUBLANES
    n_halo = T // SUBLANES

    def blk(j):
        return (nb - 1 - j) if reverse else j

    main = pl.BlockSpec((None, tb, W), lambda b, j: (b, blk(j), 0))
    in_specs = [
        main,
        pl.BlockSpec((None, SUBLANES, W), lambda b, j: (b, jnp.maximum(blk(j) * halo_per_blk - 1, 0), 0)),
        pl.BlockSpec((None, SUBLANES, W), lambda b, j: (b, jnp.minimum((blk(j) + 1) * halo_per_blk, n_halo - 1), 0)),
        pl.BlockSpec((None, 1, W), lambda b, j: (b, 0, 0)),
        _resident((CONV_W, W)), _resident((1, W)), _resident((2, W // 2, W)),
        _resident((1, W)), _resident((1, W)), _resident((1, W)),
    ]
    args = [xb, xb, xb, h0, cw, cb, w, ba, bx, lam]
    fuse = fuse_with is not None
    if fuse:
        in_specs += [main, main]
        args += list(fuse_with)
    return pl.pallas_call(
        functools.partial(_lru_kernel, n_blocks=nb, reverse=reverse, fuse=fuse),
        grid=(B, nb),
        in_specs=in_specs,
        out_specs=[main, pl.BlockSpec((None, 1, W), lambda b, j: (b, 0, 0))],
        out_shape=[jax.ShapeDtypeStruct((B, T, W), BF16 if fuse else F32),
                   jax.ShapeDtypeStruct((B, 1, W), F32)],
        scratch_shapes=[pltpu.VMEM((1, W), F32)],
        compiler_params=_cparams(2),
        name="lru_bwd" if reverse else "lru_fwd",
    )(*args)


def _merge_kernel(*refs, final):
    (x_ref, ya_ref, yb_ref, yc_ref, gt_ref, g1_ref, n2_ref, sh2_ref, sc2_ref, g2_ref,
     wb_ref, wo_ref, w1_ref, w2_ref) = refs[:14]
    if final:
        fg_ref, o_ref = refs[14:]
    else:
        (o_ref,) = refs[14:]
    D = x_ref.shape[1]
    m = None
    row = 0
    for i, y_ref in enumerate((ya_ref, yb_ref, yc_ref)):
        wdt = y_ref.shape[1]
        t = jax.nn.sigmoid(gt_ref[:, i * D:(i + 1) * D]) * jnp.dot(
            y_ref[...], wb_ref[row:row + wdt, :], preferred_element_type=F32)
        m = t if m is None else m + t
        row += wdt
    x1 = x_ref[...] + g1_ref[...] * jnp.dot(m.astype(BF16), wo_ref[...], preferred_element_type=F32)
    h2 = (_rms(x1) * n2_ref[...]) * (1.0 + sc2_ref[...]) + sh2_ref[...]
    hb = h2.astype(BF16)
    d_ff = w1_ref.shape[1]
    acc = None
    for c0 in range(0, d_ff, 2 * COL_CHUNK):
        f = jnp.dot(hb, w1_ref[:, c0:c0 + 2 * COL_CHUNK], preferred_element_type=F32)
        f = jnp.square(jnp.maximum(f, 0.0)).astype(BF16)
        t = jnp.dot(f, w2_ref[c0:c0 + 2 * COL_CHUNK, :], preferred_element_type=F32)
        acc = t if acc is None else acc + t
    x2 = x1 + g2_ref[...] * acc
    if final:
        x2 = _rms(x2) * fg_ref[...]
    o_ref[...] = x2


def _merge_call(x, ya, yb, yc, gates, mods, n2g, wb, wo, w1, w2, tokens_per_seq, cond_row, final_g=None):
    N, D = x.shape
    tm = min(TM_PROJ, tokens_per_seq)
    bps = tokens_per_seq // tm
    g1, sh2, sc2, g2 = mods

    def tok(width):
        return pl.BlockSpec((tm, width), lambda i: (i, 0))

    def cond():
        return pl.BlockSpec((None, 1, D), lambda i: (cond_row(i, bps), 0, 0))

    in_specs = [tok(D), tok(ya.shape[1]), tok(yb.shape[1]), tok(yc.shape[1]), tok(gates.shape[1]),
                cond(), _resident((1, D)), cond(), cond(), cond(),
                _resident(wb.shape), _resident(wo.shape), _resident(w1.shape), _resident(w2.shape)]
    args = [x, ya, yb, yc, gates, g1, n2g, sh2, sc2, g2, wb, wo, w1, w2]
    final = final_g is not None
    if final:
        in_specs.append(_resident((1, D)))
        args.append(final_g)
    return pl.pallas_call(
        functools.partial(_merge_kernel, final=final),
        grid=(N // tm,),
        in_specs=in_specs,
        out_specs=tok(D),
        out_shape=jax.ShapeDtypeStruct((N, D), F32),
        compiler_params=_cparams(1),
        name="merge_mlp_final" if final else "merge_mlp",
    )(*args)


def _rope_tables(S):
    t = np.arange(S)
    row = (t // GRID_W).astype(np.float32)
    col = (t % GRID_W).astype(np.float32)
    n_freq = HEAD_DIM // 4
    inv_freq = jnp.asarray(ROPE_BASE, F32) ** (-jnp.arange(n_freq, dtype=F32) / n_freq)
    ang = jnp.concatenate([row[:, None] * inv_freq, col[:, None] * inv_freq], axis=-1)
    cos, sin = jnp.cos(ang), jnp.sin(ang)
    cos_t = jnp.tile(cos, (1, LANES // (HEAD_DIM // 2)))
    sin_t = jnp.tile(jnp.concatenate([-sin, sin], axis=-1), (1, LANES // HEAD_DIM))
    return cos_t, sin_t


def _proj_weights(w_in):
    L, D, _ = w_in.shape
    sizes = (512, 512, 512, 512, 512, 512, 128, 128, 1024, 1024, 1024)
    offs = np.cumsum((0,) + sizes)
    seg = [w_in[:, :, offs[i]:offs[i + 1]] for i in range(len(sizes))]
    scale = HEAD_DIM ** -0.5

    def dup(w):
        w = w.reshape(L, D, SWA_KV_HEADS, 1, HEAD_DIM)
        return jnp.broadcast_to(w, (L, D, SWA_KV_HEADS, 2, HEAD_DIM)).reshape(L, D, 2 * SWA_KV_HEADS * HEAD_DIM)

    cols = [seg[0] * scale, seg[1], seg[2], seg[3], seg[4], seg[5] * scale, dup(seg[6]), dup(seg[7]),
            seg[8], seg[9], seg[10]]
    return jnp.concatenate(cols, axis=-1).astype(BF16)


def _lru_weights(wa, wx):
    L = wa.shape[0]
    per = LRU_BLOCKS // 2
    eye = jnp.eye(per, dtype=wa.dtype)

    def dense(w):
        w = w.reshape(L, 2, 2, per, HEAD_DIM, HEAD_DIM)
        d = jnp.einsum("ldgncf,nm->ldgncmf", w, eye)
        return d.reshape(L, 2, 2, per * HEAD_DIM, per * HEAD_DIM)

    return jnp.concatenate([dense(wa), dense(wx)], axis=-1).astype(BF16)


def _na_bias_tables(rpb):
    L, H, n_dr, n_dc = rpb.shape
    period = LANES
    v = jnp.pad(rpb, ((0, 0), (0, 0), (0, 0), (0, period - n_dc)))
    v = jnp.roll(v, -(NA_KW - 1), axis=-1)
    flat = jnp.tile(v, (1, 1, 1, GRID_W))[..., :GRID_W * (period - 1)]
    toe = flat.reshape(L, H, n_dr, GRID_W, period - 1)[..., :GRID_W]
    col = np.arange(GRID_W)
    start = np.clip(col - NA_KW // 2, 0, GRID_W - NA_KW)
    valid = (col[None, :] >= start[:, None]) & (col[None, :] < start[:, None] + NA_KW)
    toe = jnp.where(valid, toe, MASK_VALUE)
    toe = jnp.concatenate([toe, toe], axis=-1)
    return toe.reshape(L, H // 2, 2, n_dr, GRID_W, LANES)


def _swa_band():
    q = np.arange(SWA_BLK)[:, None]
    k = np.arange(3 * SWA_BLK)[None, :] - SWA_BLK
    return jnp.asarray(np.where(np.abs(k - q) <= SWA_WINDOW, 0.0, MASK_VALUE), F32)


def _sink_cols(sink, tq):
    L = sink.shape[0]
    group = SWA_Q_HEADS // SWA_KV_HEADS
    s = sink.astype(F32).reshape(L, SWA_KV_HEADS, group, 1, 1)
    return jnp.broadcast_to(s, (L, SWA_KV_HEADS, group, tq, LANES)).reshape(L, SWA_KV_HEADS, group * tq, LANES)


def kernel(x, c, ctx, c_ctx, w_mod, b_mod, norm1_g, norm2_g, w_in, na_rpb, conv_w, conv_b, lru_wa, lru_ba,
           lru_wx, lru_bx, lru_lambda, swa_sink, w_branch, w_out, w_ff1, w_ff2, final_g):
    B, S, D = x.shape
    C = ctx.shape[1]
    L = w_mod.shape[0]
    W = lru_lambda.shape[-1]
    assert B + 1 <= SUBLANES and S % (NA_QROWS * GRID_W) == 0 and S // GRID_W >= NA_KROWS
    assert C % SUBLANES == 0 and D == NA_HEADS * HEAD_DIM * 2

    cond = jnp.concatenate([c, c_ctx[None, :], jnp.zeros((SUBLANES - B - 1, D), x.dtype)], axis=0)
    mod = _mod_call(cond, w_mod, b_mod)
    mod = mod.reshape(L, SUBLANES, 6, 1, D).transpose(0, 2, 1, 3, 4)

    w_proj = _proj_weights(w_in)
    w_lru = _lru_weights(lru_wa, lru_wx)
    tz = _na_bias_tables(na_rpb)
    band = _swa_band()
    sink_lat = _sink_cols(swa_sink, SWA_BLK)
    sink_ctx = _sink_cols(swa_sink, C)
    rope_tabs = _rope_tables(S)
    wb_all, wo_all = w_branch.astype(BF16), w_out.astype(BF16)
    w1_all, w2_all = w_ff1.astype(BF16), w_ff2.astype(BF16)

    def lat_row(i, blocks_per_seq):
        return i // blocks_per_seq

    def ctx_row(i, blocks_per_seq):
        return B

    xl = x.reshape(B * S, D)
    xc = ctx.reshape(B * C, D)
    zeros_state = jnp.zeros((B, 1, W), F32)
    for l in range(L):
        update_ctx = l < L - 1
        sh1, sc1, g1, sh2, sc2, g2 = (mod[l, i] for i in range(6))
        n1g, n2g = norm1_g[l][None, :], norm2_g[l][None, :]

        qa, ka, va, xb, gb, qs, ks, vs, gates = _inproj_call(xl, sh1, sc1, n1g, w_proj[l], rope_tabs, S, lat_row)
        qac, kac, vac, xbc, gbc, qsc, ksc, vsc, gatesc = _inproj_call(xc, sh1, sc1, n1g, w_proj[l], None, C, ctx_row)

        y_a = _na_call(qa, ka, va, kac, vac, tz[l], B, S, C)
        y_c = _swa_call(qs, ks, vs, ksc, vsc, band, sink_lat[l], B, S, C)

        lru = (conv_w[l], conv_b[l][None, :])
        fwd = lru + (w_lru[l, 0], lru_ba[l, 0][None, :], lru_bx[l, 0][None, :], lru_lambda[l, 0][None, :])
        bwd = lru + (w_lru[l, 1], lru_ba[l, 1][None, :], lru_bx[l, 1][None, :], lru_lambda[l, 1][None, :])
        xbc3, gbc3 = xbc.reshape(B, C, W), gbc.reshape(B, C, W)
        xb3, gb3 = xb.reshape(B, S, W), gb.reshape(B, S, W)
        hf_c, st_f = _lru_call(xbc3, zeros_state, *fwd, reverse=False)
        y_bc, st_b = _lru_call(xbc3, zeros_state, *bwd, reverse=True, fuse_with=(hf_c, gbc3))
        hf_l, _ = _lru_call(xb3, st_f, *fwd, reverse=False)
        y_b, _ = _lru_call(xb3, st_b, *bwd, reverse=True, fuse_with=(hf_l, gb3))

        mlp_w = (wb_all[l], wo_all[l], w1_all[l], w2_all[l])
        xl = _merge_call(xl, y_a, y_b.reshape(B * S, W), y_c, gates, (g1, sh2, sc2, g2), n2g, *mlp_w, S, lat_row,
                         final_g=None if update_ctx else final_g[None, :])
        if update_ctx:
            y_ac = _na_ctx_call(qac, kac, vac, B, C)
            y_cc = _swa_ctx_call(qsc, ksc, vsc, sink_ctx[l], B, C)
            xc = _merge_call(xc, y_ac, y_bc.reshape(B * C, W), y_cc, gatesc, (g1, sh2, sc2, g2), n2g, *mlp_w, C,
                             ctx_row)
    return xl.reshape(B, S, D)
```

```python
import functools

import numpy as np
import jax
import jax.numpy as jnp
from jax import lax
from jax.experimental import pallas as pl
from jax.experimental.pallas import tpu as pltpu

F32 = jnp.float32
BF16 = jnp.bfloat16

HEAD_DIM = 64
GRID_W = 64
NA_HEADS = 8
NA_KH = 8
NA_KW = 16
LRU_BLOCKS = 8
LRU_C = 8.0
CONV_W = 4
SWA_Q_HEADS = 8
SWA_KV_HEADS = 2
SWA_WINDOW = 128
ROPE_BASE = 10000.0
NORM_EPS = 1e-6
MASK_VALUE = -1e30

LANES = 128
SUBLANES = 8
V7X_VMEM_BYTES = 64 * 1024 * 1024
VMEM_LIMIT = V7X_VMEM_BYTES - 8 * 1024 * 1024

TM_PROJ = 512
NA_QROWS = 4
NA_KROWS = 12
NA_KBLK = 4
SWA_BLK = 128
LRU_TB = 256
COL_CHUNK = 512


def _cparams(n_axes):
    return pltpu.CompilerParams(dimension_semantics=("arbitrary",) * n_axes,
                                vmem_limit_bytes=VMEM_LIMIT)


def _resident(shape):
    return pl.BlockSpec(shape, lambda *_: (0,) * len(shape), pipeline_mode=pl.Buffered(1))


def _rms(x):
    return x * lax.rsqrt(jnp.mean(x * x, axis=-1, keepdims=True) + NORM_EPS)


def _mod_kernel(c_ref, w_ref, b_ref, o_ref):
    c = c_ref[...]
    s = c * jax.nn.sigmoid(c)
    o_ref[...] = jnp.dot(s, w_ref[...], preferred_element_type=F32) + b_ref[...]


def _mod_call(cond, w_mod, b_mod):
    L, D, D6 = w_mod.shape
    n = D6 // D
    return pl.pallas_call(
        _mod_kernel,
        grid=(L, n),
        in_specs=[
            pl.BlockSpec((SUBLANES, D), lambda l, j: (0, 0)),
            pl.BlockSpec((None, D, D), lambda l, j: (l, 0, j)),
            pl.BlockSpec((None, 1, D), lambda l, j: (l, 0, j)),
        ],
        out_specs=pl.BlockSpec((None, SUBLANES, D), lambda l, j: (l, 0, j)),
        out_shape=jax.ShapeDtypeStruct((L, SUBLANES, D6), F32),
        compiler_params=_cparams(2),
        name="mod",
    )(cond, w_mod, b_mod.reshape(L, 1, D6))


_PROJ_SEGS = (
    ("qa", 512, BF16, False), ("ka", 512, BF16, False),
    ("xb", 512, F32, False), ("gb", 512, F32, False),
    ("qs", 512, BF16, True), ("ks", 256, BF16, True),
    ("gates", 3072, F32, False),
)
_PROJ_WIDTH = sum(s[1] for s in _PROJ_SEGS)
_PROJT_SEGS = (("vaT", 512, NA_KBLK * GRID_W), ("vsT", 128, SWA_BLK))
_PROJT_WIDTH = sum(s[1] for s in _PROJT_SEGS)


def _rope(z, cos, sin):
    w = z.shape[1]
    lane = lax.broadcasted_iota(jnp.int32, (1, w), 1)
    first_half = (lane & (HEAD_DIM - 1)) < HEAD_DIM // 2
    swapped = jnp.where(first_half, pltpu.roll(z, w - HEAD_DIM // 2, 1), pltpu.roll(z, HEAD_DIM // 2, 1))
    reps = w // LANES
    return z * jnp.tile(cos, (1, reps)) + swapped * jnp.tile(sin, (1, reps))


def _inproj_kernel(*refs, rope):
    n_out = len(_PROJ_SEGS) + len(_PROJT_SEGS)
    if rope:
        x_ref, shift_ref, scale_ref, g_ref, cos_ref, sin_ref, w_ref, wt_ref = refs[:-n_out]
    else:
        x_ref, shift_ref, scale_ref, g_ref, w_ref, wt_ref = refs[:-n_out]
    outs = refs[-n_out:]
    h = (_rms(x_ref[...]) * g_ref[...]) * (1.0 + scale_ref[...]) + shift_ref[...]
    hb = h.astype(BF16)
    col = 0
    for (_, width, _, rotary), o_ref in zip(_PROJ_SEGS, outs):
        for c0 in range(0, width, COL_CHUNK):
            cw = min(COL_CHUNK, width - c0)
            z = jnp.dot(hb, w_ref[:, col + c0:col + c0 + cw], preferred_element_type=F32)
            if rotary and rope:
                z = _rope(z, cos_ref[...], sin_ref[...])
            o_ref[:, c0:c0 + cw] = z.astype(o_ref.dtype)
        col += width
    row = 0
    for (_, feats, tblk), o_ref in zip(_PROJT_SEGS, outs[len(_PROJ_SEGS):]):
        for t in range(o_ref.shape[0]):
            z = _qk(wt_ref[row:row + feats, :], hb[t * tblk:(t + 1) * tblk])
            o_ref[t] = z.astype(o_ref.dtype)
        row += feats


def _inproj_call(x, shift, scale, g, w, wt, rope_tabs, tokens_per_seq, cond_row):
    N, D = x.shape
    tm = min(TM_PROJ, tokens_per_seq)
    blocks_per_seq = tokens_per_seq // tm
    rope = rope_tabs is not None
    in_specs = [
        pl.BlockSpec((tm, D), lambda i: (i, 0)),
        pl.BlockSpec((None, 1, D), lambda i: (cond_row(i, blocks_per_seq), 0, 0)),
        pl.BlockSpec((None, 1, D), lambda i: (cond_row(i, blocks_per_seq), 0, 0)),
        _resident((1, D)),
    ]
    args = [x, shift, scale, g]
    if rope:
        in_specs += [pl.BlockSpec((tm, LANES), lambda i: (i % blocks_per_seq, 0))] * 2
        args += list(rope_tabs)
    in_specs += [_resident((D, _PROJ_WIDTH)), _resident((_PROJT_WIDTH, D))]
    args += [w, wt]
    out_specs = [pl.BlockSpec((tm, width), lambda i: (i, 0)) for _, width, _, _ in _PROJ_SEGS]
    out_shape = [jax.ShapeDtypeStruct((N, width), dt) for _, width, dt, _ in _PROJ_SEGS]
    for _, feats, tblk in _PROJT_SEGS:
        out_specs.append(pl.BlockSpec((tm // tblk, feats, tblk), lambda i: (i, 0, 0)))
        out_shape.append(jax.ShapeDtypeStruct((N // tblk, feats, tblk), BF16))
    return pl.pallas_call(
        functools.partial(_inproj_kernel, rope=rope),
        grid=(N // tm,),
        in_specs=in_specs,
        out_specs=out_specs,
        out_shape=out_shape,
        compiler_params=_cparams(1),
        name="inproj_rope" if rope else "inproj",
    )(*args)


def _qk(a, b):
    return lax.dot_general(a, b, (((1,), (1,)), ((), ())), preferred_element_type=F32)


def _low_half():
    return lax.broadcasted_iota(jnp.int32, (1, LANES), 1) < HEAD_DIM


def _softmax_pv_t(blocks, sink=None):
    mx = None
    for s, _ in blocks:
        r = jnp.max(s, axis=0, keepdims=True)
        mx = r if mx is None else jnp.maximum(mx, r)
    if sink is not None:
        mx = jnp.maximum(mx, sink)
    denom = None if sink is None else jnp.exp(sink - mx)
    acc = None
    for s, vparts in blocks:
        p = jnp.exp(s - mx)
        ps = jnp.sum(p, axis=0, keepdims=True)
        denom = ps if denom is None else denom + ps
        pb = p.astype(BF16)
        for vt, k0, k1 in vparts:
            pv = jnp.dot(vt, pb[k0:k1], preferred_element_type=F32)
            acc = pv if acc is None else acc + pv
    return acc * (1.0 / denom)


def _na_bias_plan(tile_type):
    plan = {}
    for a in range(NA_QROWS):
        for b in range(NA_KROWS):
            if tile_type == 0:
                first, off = 0, NA_KH - 1
            elif tile_type == 1:
                first, off = a, NA_KH // 2 - 1
            else:
                first, off = NA_KROWS - NA_KH, -1
            plan[(a, b)] = b - a + off if first <= b < first + NA_KH else None
    return plan


def _na_assemble_bias(tile_type, tz_ref, bias_scr, low):
    plan = _na_bias_plan(tile_type)
    masked = jnp.full((GRID_W, LANES), MASK_VALUE, F32)
    for e in range(2):
        for b in range(NA_KROWS):
            for aa in range(NA_QROWS // 2):
                d0, d1 = plan[(2 * aa, b)], plan[(2 * aa + 1, b)]
                if d0 is None and d1 is None:
                    val = masked
                else:
                    t0 = masked if d0 is None else tz_ref[e, d0]
                    t1 = masked if d1 is None else tz_ref[e, d1]
                    val = jnp.where(low, t0, t1)
                bias_scr[e, b * GRID_W:(b + 1) * GRID_W, aa * LANES:(aa + 1) * LANES] = val


def _na_kernel(*refs, n_tiles, local):
    if local:
        q_ref, k_ref, vt_ref, kc_ref, vct_ref, tz_ref, o_ref, bias_scr = refs
    else:
        q_ref, kc_ref, vct_ref, o_ref = refs
    low = _low_half()
    kblk = NA_KBLK * GRID_W
    nwin = NA_KROWS // NA_KBLK
    if local:
        m = pl.program_id(2)

        @pl.when(m == 0)
        def _():
            _na_assemble_bias(0, tz_ref, bias_scr, low)

        @pl.when(m == 1)
        def _():
            _na_assemble_bias(1, tz_ref, bias_scr, low)

        @pl.when(m == n_tiles - 1)
        def _():
            _na_assemble_bias(2, tz_ref, bias_scr, low)

        kb0 = jnp.clip(m - 1, 0, n_tiles - nwin)
        kwin = k_ref[pl.ds(pl.multiple_of(kb0 * kblk, kblk), nwin * kblk), :]
        keys = jnp.concatenate([kwin, kc_ref[...]], axis=0)
    else:
        keys = kc_ref[...]
    q = q_ref[...]
    scores = []
    for e in range(2):
        sel = low if e == 0 else jnp.logical_not(low)
        scores.append(_qk(keys, jnp.where(sel, q, jnp.zeros_like(q))))
    n_ctx = kc_ref.shape[0]
    for e in range(2):
        hs = slice(e * HEAD_DIM, (e + 1) * HEAD_DIM)
        s = scores[e]
        blocks = []
        if local:
            n_loc = nwin * kblk
            vparts = [(vt_ref[kb0 + j, hs, :], j * kblk, (j + 1) * kblk) for j in range(nwin)]
            blocks.append((s[:n_loc] + bias_scr[e], vparts))
            s = s[n_loc:]
        blocks.append((s, [(vct_ref[hs, :], 0, n_ctx)]))
        o_ref[hs, :] = _softmax_pv_t(blocks).astype(o_ref.dtype)


def _na_call(q, k, vt, kc, vct, tz, B, S, C):
    rows = S // GRID_W
    n_tiles = rows // NA_QROWS
    tq = NA_QROWS * GRID_W
    kblk = NA_KBLK * GRID_W
    pairs = NA_HEADS // 2
    in_specs = [
        pl.BlockSpec((tq, LANES), lambda b, p, m: (b * n_tiles + m, p)),
        pl.BlockSpec((S, LANES), lambda b, p, m: (b, p)),
        pl.BlockSpec((S // kblk, LANES, kblk), lambda b, p, m: (b, p, 0)),
        pl.BlockSpec((C, LANES), lambda b, p, m: (b, p)),
        pl.BlockSpec((None, LANES, C), lambda b, p, m: (b, p, 0)),
        pl.BlockSpec((None, 2, 2 * NA_KH - 1, GRID_W, LANES), lambda b, p, m: (p, 0, 0, 0, 0)),
    ]
    return pl.pallas_call(
        functools.partial(_na_kernel, n_tiles=n_tiles, local=True),
        grid=(B, pairs, n_tiles),
        in_specs=in_specs,
        out_specs=pl.BlockSpec((LANES, tq), lambda b, p, m: (p, b * n_tiles + m)),
        out_shape=jax.ShapeDtypeStruct((NA_HEADS * HEAD_DIM, B * S), BF16),
        scratch_shapes=[pltpu.VMEM((2, NA_KROWS * GRID_W, tq), F32)],
        compiler_params=_cparams(3),
        name="na_attn",
    )(q, k, vt, kc, vct, tz)


def _na_ctx_call(qc, kc, vct, B, C):
    pairs = NA_HEADS // 2
    spec = pl.BlockSpec((C, LANES), lambda b, p: (b, p))
    return pl.pallas_call(
        functools.partial(_na_kernel, n_tiles=1, local=False),
        grid=(B, pairs),
        in_specs=[spec, spec, pl.BlockSpec((None, LANES, C), lambda b, p: (b, p, 0))],
        out_specs=pl.BlockSpec((LANES, C), lambda b, p: (p, b)),
        out_shape=jax.ShapeDtypeStruct((NA_HEADS * HEAD_DIM, B * C), BF16),
        compiler_params=_cparams(2),
        name="na_ctx_attn",
    )(qc, kc, vct)


def _swa_kernel(*refs, n_blocks, local):
    if local:
        q_ref, k_ref, vt_ref, kc_ref, vct_ref, band_ref, sink_ref, o_ref = refs
    else:
        q_ref, kc_ref, vct_ref, sink_ref, o_ref = refs
    low = _low_half()
    tq = q_ref.shape[0]
    group = SWA_Q_HEADS // SWA_KV_HEADS
    n_ctx = kc_ref.shape[0]
    if local:
        n = pl.program_id(1)
        w0 = jnp.clip(n - 1, 0, n_blocks - 3)
        band_type = jnp.where(n == 0, 0, jnp.where(n == n_blocks - 1, 2, 1))
        bias = jnp.tile(band_ref[band_type], (1, group))
        kwin = k_ref[pl.ds(pl.multiple_of(w0 * SWA_BLK, SWA_BLK), 3 * SWA_BLK), :]
    scores = []
    for g in range(SWA_KV_HEADS):
        parts = []
        for r in range(group):
            h = g * group + r
            qp = q_ref[:, (h // 2) * LANES:(h // 2 + 1) * LANES]
            sel = low if h % 2 == 0 else jnp.logical_not(low)
            parts.append(jnp.where(sel, qp, jnp.zeros_like(qp)))
        qst = jnp.concatenate(parts, axis=0)
        gl = slice(g * LANES, (g + 1) * LANES)
        keys = kc_ref[:, gl]
        if local:
            keys = jnp.concatenate([kwin[:, gl], keys], axis=0)
        scores.append(_qk(keys, qst))
    for g in range(SWA_KV_HEADS):
        gs = slice(g * HEAD_DIM, (g + 1) * HEAD_DIM)
        s = scores[g]
        blocks = []
        if local:
            vparts = [(vt_ref[w0 + j, gs, :], j * SWA_BLK, (j + 1) * SWA_BLK) for j in range(3)]
            blocks.append((s[:3 * SWA_BLK] + bias, vparts))
            s = s[3 * SWA_BLK:]
        blocks.append((s, [(vct_ref[j, gs, :], j * SWA_BLK, (j + 1) * SWA_BLK) for j in range(n_ctx // SWA_BLK)]))
        o = _softmax_pv_t(blocks, sink=sink_ref[g][0:1, :])
        for r in range(group):
            h = g * group + r
            o_ref[h * HEAD_DIM:(h + 1) * HEAD_DIM, :] = o[:, r * tq:(r + 1) * tq].astype(o_ref.dtype)


def _swa_call(q, k, vt, kc, vct, band, sink, B, S, C):
    nb = S // SWA_BLK
    in_specs = [
        pl.BlockSpec((SWA_BLK, q.shape[1]), lambda b, n: (b * nb + n, 0)),
        pl.BlockSpec((S, k.shape[1]), lambda b, n: (b, 0)),
        pl.BlockSpec((nb,) + vt.shape[1:], lambda b, n: (b, 0, 0)),
        pl.BlockSpec((C, kc.shape[1]), lambda b, n: (b, 0)),
        pl.BlockSpec((C // SWA_BLK,) + vct.shape[1:], lambda b, n: (b, 0, 0)),
        pl.BlockSpec(band.shape, lambda b, n: (0, 0, 0)),
        pl.BlockSpec(sink.shape, lambda b, n: (0, 0, 0)),
    ]
    return pl.pallas_call(
        functools.partial(_swa_kernel, n_blocks=nb, local=True),
        grid=(B, nb),
        in_specs=in_specs,
        out_specs=pl.BlockSpec((SWA_Q_HEADS * HEAD_DIM, SWA_BLK), lambda b, n: (0, b * nb + n)),
        out_shape=jax.ShapeDtypeStruct((SWA_Q_HEADS * HEAD_DIM, B * S), BF16),
        compiler_params=_cparams(2),
        name="swa_attn",
    )(q, k, vt, kc, vct, band, sink)


def _swa_ctx_call(qc, kc, vct, sink, B, C):
    return pl.pallas_call(
        functools.partial(_swa_kernel, n_blocks=1, local=False),
        grid=(B,),
        in_specs=[pl.BlockSpec((C, qc.shape[1]), lambda b: (b, 0)),
                  pl.BlockSpec((C, kc.shape[1]), lambda b: (b, 0)),
                  pl.BlockSpec((C // SWA_BLK,) + vct.shape[1:], lambda b: (b, 0, 0)),
                  pl.BlockSpec(sink.shape, lambda b: (0, 0, 0))],
        out_specs=pl.BlockSpec((SWA_Q_HEADS * HEAD_DIM, C), lambda b: (0, b)),
        out_shape=jax.ShapeDtypeStruct((SWA_Q_HEADS * HEAD_DIM, B * C), BF16),
        compiler_params=_cparams(1),
        name="swa_ctx_attn",
    )(qc, kc, vct, sink)


def _scan_block(a, b, carry, reverse):
    tb = a.shape[0]
    row = lax.broadcasted_iota(jnp.int32, (tb, 1), 0) & (SUBLANES - 1)
    A, Bv = a, b
    for s in (1, 2, 4):
        if reverse:
            inside, shift = row < SUBLANES - s, tb - s
        else:
            inside, shift = row >= s, s
        a_sh = jnp.where(inside, pltpu.roll(A, shift, 0), 1.0)
        b_sh = jnp.where(inside, pltpu.roll(Bv, shift, 0), 0.0)
        Bv = A * b_sh + Bv
        A = A * a_sh
    n_groups = tb // SUBLANES
    hs = [None] * n_groups
    for k in (reversed(range(n_groups)) if reverse else range(n_groups)):
        hk = A[k * SUBLANES:(k + 1) * SUBLANES] * carry + Bv[k * SUBLANES:(k + 1) * SUBLANES]
        hs[k] = hk
        carry = hk[0:1] if reverse else hk[SUBLANES - 1:SUBLANES]
    return jnp.concatenate(hs, axis=0), carry


def _lru_kernel(*refs, n_blocks, reverse, fuse):
    x_ref, xp_ref, xn_ref, h0_ref, cw_ref, cb_ref, w_ref, ba_ref, bx_ref, lam_ref = refs[:10]
    if fuse:
        hf_ref, gb_ref, y_ref, st_ref, carry_ref = refs[10:]
    else:
        y_ref, st_ref, carry_ref = refs[10:]
    j = pl.program_id(1)
    blk = (n_blocks - 1 - j) if reverse else j

    @pl.when(j == 0)
    def _():
        carry_ref[...] = h0_ref[...]

    x = x_ref[...]
    tb = x.shape[0]
    xp = jnp.where(blk > 0, xp_ref[...], 0.0)
    xn = jnp.where(blk < n_blocks - 1, xn_ref[...], 0.0)
    xc = jnp.concatenate([xp, x, xn], axis=0)
    cw = cw_ref[...]
    base = SUBLANES - CONV_W // 2
    u = xc[base:base + tb] * cw[0:1]
    for i in range(1, CONV_W):
        u = u + xc[base + i:base + i + tb] * cw[i:i + 1]
    u = u + cb_ref[...]
    ub = u.astype(BF16)

    z = -lam_ref[...]
    softplus = jnp.maximum(z, 0.0) + jnp.log1p(jnp.exp(-jnp.abs(z)))
    half = x.shape[1] // 2
    for g in range(2):
        cols = slice(g * half, (g + 1) * half)
        zz = jnp.dot(ub[:, cols], w_ref[g], preferred_element_type=F32)
        r = jax.nn.sigmoid(zz[:, :half] + ba_ref[:, cols])
        gate_i = jax.nn.sigmoid(zz[:, half:] + bx_ref[:, cols])
        log_a = (-LRU_C * r) * softplus[:, cols]
        a = jnp.exp(log_a)
        bcoef = jnp.sqrt(1.0 - a * a) * gate_i * u[:, cols]
        h, carry = _scan_block(a, bcoef, carry_ref[:, cols], reverse)
        carry_ref[:, cols] = carry
        st_ref[:, cols] = carry
        if fuse:
            y_ref[:, cols] = ((hf_ref[:, cols] + h) * jax.nn.gelu(gb_ref[:, cols])).astype(y_ref.dtype)
        else:
            y_ref[:, cols] = h


def _lru_call(xb, h0, cw, cb, w, ba, bx, lam, reverse, fuse_with=None):
    B, T, W = xb.shape
    tb = min(LRU_TB, T)
    nb = T // tb
    halo_per_blk = tb // SUBLANES
    n_halo = T // SUBLANES

    def blk(j):
        return (nb - 1 - j) if reverse else j

    main = pl.BlockSpec((None, tb, W), lambda b, j: (b, blk(j), 0))
    in_specs = [
        main,
        pl.BlockSpec((None, SUBLANES, W), lambda b, j: (b, jnp.maximum(blk(j) * halo_per_blk - 1, 0), 0)),
        pl.BlockSpec((None, SUBLANES, W), lambda b, j: (b, jnp.minimum((blk(j) + 1) * halo_per_blk, n_halo - 1), 0)),
        pl.BlockSpec((None, 1, W), lambda b, j: (b, 0, 0)),
        _resident((CONV_W, W)), _resident((1, W)), _resident((2, W // 2, W)),
        _resident((1, W)), _resident((1, W)), _resident((1, W)),
    ]
    args = [xb, xb, xb, h0, cw, cb, w, ba, bx, lam]
    fuse = fuse_with is not None
    if fuse:
        in_specs += [main, main]
        args += list(fuse_with)
    return pl.pallas_call(
        functools.partial(_lru_kernel, n_blocks=nb, reverse=reverse, fuse=fuse),
        grid=(B, nb),
        in_specs=in_specs,
        out_specs=[main, pl.BlockSpec((None, 1, W), lambda b, j: (b, 0, 0))],
        out_shape=[jax.ShapeDtypeStruct((B, T, W), BF16 if fuse else F32),
                   jax.ShapeDtypeStruct((B, 1, W), F32)],
        scratch_shapes=[pltpu.VMEM((1, W), F32)],
        compiler_params=_cparams(2),
        name="lru_bwd" if reverse else "lru_fwd",
    )(*args)


def _merge_kernel(*refs, final):
    (x_ref, ya_ref, yb_ref, yc_ref, gt_ref, g1_ref, n2_ref, sh2_ref, sc2_ref, g2_ref,
     wb_ref, wo_ref, w1_ref, w2_ref) = refs[:14]
    if final:
        fg_ref, o_ref = refs[14:]
    else:
        (o_ref,) = refs[14:]
    D = x_ref.shape[1]
    m = None
    row = 0
    for i, (y_ref, feature_major) in enumerate(((ya_ref, True), (yb_ref, False), (yc_ref, True))):
        wdt = y_ref.shape[0] if feature_major else y_ref.shape[1]
        contract = (((0,), (0,)), ((), ())) if feature_major else (((1,), (0,)), ((), ()))
        t = jax.nn.sigmoid(gt_ref[:, i * D:(i + 1) * D]) * lax.dot_general(
            y_ref[...], wb_ref[row:row + wdt, :], contract, preferred_element_type=F32)
        m = t if m is None else m + t
        row += wdt
    x1 = x_ref[...] + g1_ref[...] * jnp.dot(m.astype(BF16), wo_ref[...], preferred_element_type=F32)
    h2 = (_rms(x1) * n2_ref[...]) * (1.0 + sc2_ref[...]) + sh2_ref[...]
    hb = h2.astype(BF16)
    d_ff = w1_ref.shape[1]
    acc = None
    for c0 in range(0, d_ff, 2 * COL_CHUNK):
        f = jnp.dot(hb, w1_ref[:, c0:c0 + 2 * COL_CHUNK], preferred_element_type=F32)
        f = jnp.square(jnp.maximum(f, 0.0)).astype(BF16)
        t = jnp.dot(f, w2_ref[c0:c0 + 2 * COL_CHUNK, :], preferred_element_type=F32)
        acc = t if acc is None else acc + t
    x2 = x1 + g2_ref[...] * acc
    if final:
        x2 = _rms(x2) * fg_ref[...]
    o_ref[...] = x2


def _merge_call(x, ya, yb, yc, gates, mods, n2g, wb, wo, w1, w2, tokens_per_seq, cond_row, final_g=None):
    N, D = x.shape
    tm = min(TM_PROJ, tokens_per_seq)
    bps = tokens_per_seq // tm
    g1, sh2, sc2, g2 = mods

    def tok(width):
        return pl.BlockSpec((tm, width), lambda i: (i, 0))

    def tok_t(feats):
        return pl.BlockSpec((feats, tm), lambda i: (0, i))

    def cond():
        return pl.BlockSpec((None, 1, D), lambda i: (cond_row(i, bps), 0, 0))

    in_specs = [tok(D), tok_t(ya.shape[0]), tok(yb.shape[1]), tok_t(yc.shape[0]), tok(gates.shape[1]),
                cond(), _resident((1, D)), cond(), cond(), cond(),
                _resident(wb.shape), _resident(wo.shape), _resident(w1.shape), _resident(w2.shape)]
    args = [x, ya, yb, yc, gates, g1, n2g, sh2, sc2, g2, wb, wo, w1, w2]
    final = final_g is not None
    if final:
        in_specs.append(_resident((1, D)))
        args.append(final_g)
    return pl.pallas_call(
        functools.partial(_merge_kernel, final=final),
        grid=(N // tm,),
        in_specs=in_specs,
        out_specs=tok(D),
        out_shape=jax.ShapeDtypeStruct((N, D), F32),
        compiler_params=_cparams(1),
        name="merge_mlp_final" if final else "merge_mlp",
    )(*args)


def _rope_tables(S):
    t = np.arange(S)
    row = (t // GRID_W).astype(np.float32)
    col = (t % GRID_W).astype(np.float32)
    n_freq = HEAD_DIM // 4
    inv_freq = jnp.asarray(ROPE_BASE, F32) ** (-jnp.arange(n_freq, dtype=F32) / n_freq)
    ang = jnp.concatenate([row[:, None] * inv_freq, col[:, None] * inv_freq], axis=-1)
    cos, sin = jnp.cos(ang), jnp.sin(ang)
    cos_t = jnp.tile(cos, (1, LANES // (HEAD_DIM // 2)))
    sin_t = jnp.tile(jnp.concatenate([-sin, sin], axis=-1), (1, LANES // HEAD_DIM))
    return cos_t, sin_t


def _proj_weights(w_in):
    L, D, _ = w_in.shape
    sizes = (512, 512, 512, 512, 512, 512, 128, 128, 1024, 1024, 1024)
    offs = np.cumsum((0,) + sizes)
    seg = [w_in[:, :, offs[i]:offs[i + 1]] for i in range(len(sizes))]
    scale = HEAD_DIM ** -0.5

    def dup(w):
        w = w.reshape(L, D, SWA_KV_HEADS, 1, HEAD_DIM)
        return jnp.broadcast_to(w, (L, D, SWA_KV_HEADS, 2, HEAD_DIM)).reshape(L, D, 2 * SWA_KV_HEADS * HEAD_DIM)

    cols = [seg[0] * scale, seg[1], seg[3], seg[4], seg[5] * scale, dup(seg[6]), seg[8], seg[9], seg[10]]
    w = jnp.concatenate(cols, axis=-1).astype(BF16)
    wt = jnp.swapaxes(jnp.concatenate([seg[2], seg[7]], axis=-1), 1, 2).astype(BF16)
    return w, wt


def _lru_weights(wa, wx):
    L = wa.shape[0]
    per = LRU_BLOCKS // 2
    eye = jnp.eye(per, dtype=wa.dtype)

    def dense(w):
        w = w.reshape(L, 2, 2, per, HEAD_DIM, HEAD_DIM)
        d = jnp.einsum("ldgncf,nm->ldgncmf", w, eye)
        return d.reshape(L, 2, 2, per * HEAD_DIM, per * HEAD_DIM)

    return jnp.concatenate([dense(wa), dense(wx)], axis=-1).astype(BF16)


def _na_bias_tables(rpb):
    L, H, n_dr, n_dc = rpb.shape
    period = LANES
    v = jnp.pad(rpb, ((0, 0), (0, 0), (0, 0), (0, period - n_dc)))
    v = jnp.roll(v, -(NA_KW - 1), axis=-1)
    flat = jnp.tile(v, (1, 1, 1, GRID_W))[..., :GRID_W * (period - 1)]
    toe = flat.reshape(L, H, n_dr, GRID_W, period - 1)[..., :GRID_W]
    col = np.arange(GRID_W)
    start = np.clip(col - NA_KW // 2, 0, GRID_W - NA_KW)
    valid = (col[None, :] >= start[:, None]) & (col[None, :] < start[:, None] + NA_KW)
    toe = jnp.swapaxes(jnp.where(valid, toe, MASK_VALUE), -1, -2)
    toe = jnp.concatenate([toe, toe], axis=-1)
    return toe.reshape(L, H // 2, 2, n_dr, GRID_W, LANES)


def _swa_band():
    k = np.arange(3 * SWA_BLK)[None, :, None]
    q = np.arange(SWA_BLK)[None, None, :]
    shift = (np.arange(3) * SWA_BLK)[:, None, None]
    return jnp.asarray(np.where(np.abs(k - shift - q) <= SWA_WINDOW, 0.0, MASK_VALUE), F32)


def _sink_rows(sink, tq):
    L = sink.shape[0]
    group = SWA_Q_HEADS // SWA_KV_HEADS
    s = sink.astype(F32).reshape(L, SWA_KV_HEADS, 1, group, 1)
    return jnp.broadcast_to(s, (L, SWA_KV_HEADS, SUBLANES, group, tq)).reshape(L, SWA_KV_HEADS, SUBLANES, group * tq)


def kernel(x, c, ctx, c_ctx, w_mod, b_mod, norm1_g, norm2_g, w_in, na_rpb, conv_w, conv_b, lru_wa, lru_ba,
           lru_wx, lru_bx, lru_lambda, swa_sink, w_branch, w_out, w_ff1, w_ff2, final_g):
    B, S, D = x.shape
    C = ctx.shape[1]
    L = w_mod.shape[0]
    W = lru_lambda.shape[-1]
    assert B + 1 <= SUBLANES and S % TM_PROJ == 0 and S // GRID_W >= NA_KROWS and S // SWA_BLK >= 3
    assert C == NA_KBLK * GRID_W and D == NA_HEADS * HEAD_DIM * 2

    cond = jnp.concatenate([c, c_ctx[None, :], jnp.zeros((SUBLANES - B - 1, D), x.dtype)], axis=0)
    mod = _mod_call(cond, w_mod, b_mod)
    mod = mod.reshape(L, SUBLANES, 6, 1, D).transpose(0, 2, 1, 3, 4)

    w_proj, wt_proj = _proj_weights(w_in)
    w_lru = _lru_weights(lru_wa, lru_wx)
    tz = _na_bias_tables(na_rpb)
    band = _swa_band()
    sink_lat = _sink_rows(swa_sink, SWA_BLK)
    sink_ctx = _sink_rows(swa_sink, C)
    rope_tabs = _rope_tables(S)
    wb_all, wo_all = w_branch.astype(BF16), w_out.astype(BF16)
    w1_all, w2_all = w_ff1.astype(BF16), w_ff2.astype(BF16)

    def lat_row(i, blocks_per_seq):
        return i // blocks_per_seq

    def ctx_row(i, blocks_per_seq):
        return B

    xl = x.reshape(B * S, D)
    xc = ctx.reshape(B * C, D)
    zeros_state = jnp.zeros((B, 1, W), F32)
    for l in range(L):
        update_ctx = l < L - 1
        sh1, sc1, g1, sh2, sc2, g2 = (mod[l, i] for i in range(6))
        n1g, n2g = norm1_g[l][None, :], norm2_g[l][None, :]

        qa, ka, xb, gb, qs, ks, gates, vat, vst = _inproj_call(
            xl, sh1, sc1, n1g, w_proj[l], wt_proj[l], rope_tabs, S, lat_row)
        qac, kac, xbc, gbc, qsc, ksc, gatesc, vact, vsct = _inproj_call(
            xc, sh1, sc1, n1g, w_proj[l], wt_proj[l], None, C, ctx_row)

        y_a = _na_call(qa, ka, vat, kac, vact, tz[l], B, S, C)
        y_c = _swa_call(qs, ks, vst, ksc, vsct, band, sink_lat[l], B, S, C)

        lru = (conv_w[l], conv_b[l][None, :])
        fwd = lru + (w_lru[l, 0], lru_ba[l, 0][None, :], lru_bx[l, 0][None, :], lru_lambda[l, 0][None, :])
        bwd = lru + (w_lru[l, 1], lru_ba[l, 1][None, :], lru_bx[l, 1][None, :], lru_lambda[l, 1][None, :])
        xbc3, gbc3 = xbc.reshape(B, C, W), gbc.reshape(B, C, W)
        xb3, gb3 = xb.reshape(B, S, W), gb.reshape(B, S, W)
        hf_c, st_f = _lru_call(xbc3, zeros_state, *fwd, reverse=False)
        y_bc, st_b = _lru_call(xbc3, zeros_state, *bwd, reverse=True, fuse_with=(hf_c, gbc3))
        hf_l, _ = _lru_call(xb3, st_f, *fwd, reverse=False)
        y_b, _ = _lru_call(xb3, st_b, *bwd, reverse=True, fuse_with=(hf_l, gb3))

        mlp_w = (wb_all[l], wo_all[l], w1_all[l], w2_all[l])
        xl = _merge_call(xl, y_a, y_b.reshape(B * S, W), y_c, gates, (g1, sh2, sc2, g2), n2g, *mlp_w, S, lat_row,
                         final_g=None if update_ctx else final_g[None, :])
        if update_ctx:
            y_ac = _na_ctx_call(qac, kac, vact, B, C)
            y_cc = _swa_ctx_call(qsc, ksc, vsct, sink_ctx[l], B, C)
            xc = _merge_call(xc, y_ac, y_bc.reshape(B * C, W), y_cc, gatesc, (g1, sh2, sc2, g2), n2g, *mlp_w, C,
                             ctx_row)
    return xl.reshape(B, S, D)
```
